```python
import math
import jax, jax.numpy as jnp
from jax import lax
import numpy as np

D_MODEL = 2048
BATCH = 2
SEQ = 4096
DEPTH = 4
DEC_BATCH = 8
DEC_SEQ = 4
PAST_LEN = 16384
PAGE_SIZE = 128

N_EVEN = (DEPTH + 1) // 2
N_ODD = DEPTH // 2
D_CONV = D_MODEL // 2
SCONV_W = 3
N_HEADS = 8
HEAD_DK = D_MODEL // (4 * N_HEADS)
HEAD_DV = 2 * HEAD_DK
D_QK = N_HEADS * 2 * HEAD_DK
D_ATTN = N_HEADS * HEAD_DV
D_IN_EVEN = 3 * D_CONV + 2 * D_QK + D_ATTN
ROPE_THETA = 10000.0
Q_BLOCK = 128
D_RNN = D_MODEL
RG_BLOCKS = 8
RG_BW = D_RNN // RG_BLOCKS
RG_CONV_W = 4
RG_C = 8.0
N_GROUPS = 4
E_PER_GROUP = 8
N_EXPERTS = N_GROUPS * E_PER_GROUP
TOP_K_INNER = 2
D_EXPERT = 512
D_PLE = 256
EPS = 1e-6
NEG = -1e30

kernel_name = 'hybrid_conv_diffattn_rglru_hmoe_step'


def _rms_norm(x, g):
    xf = x.astype(jnp.float32)
    y = xf * lax.rsqrt(jnp.mean(xf * xf, axis=-1, keepdims=True) + EPS)
    return (y * g.astype(jnp.float32)).astype(x.dtype)


def _rope(x, pos):
    half = HEAD_DK // 2
    inv = jnp.exp(-math.log(ROPE_THETA) * jnp.arange(half, dtype=jnp.float32) * 2.0 / HEAD_DK)
    ang = pos[:, None] * inv[None, :]
    cos = jnp.cos(ang)[None, :, None, None, :]
    sin = jnp.sin(ang)[None, :, None, None, :]
    xf = x.astype(jnp.float32)
    x1, x2 = xf[..., :half], xf[..., half:]
    return jnp.concatenate([x1 * cos - x2 * sin, x2 * cos + x1 * sin], axis=-1).astype(x.dtype)


def _causal_dwconv(z, w, prev):
    width = w.shape[0]
    t = z.shape[1]
    zz = jnp.concatenate([prev.astype(z.dtype), z], axis=1)
    out = zz[:, 0:t] * w[0]
    for j in range(1, width):
        out = out + zz[:, j:j + t] * w[j]
    return out, zz[:, t:]


def _diff_attend(q, k, v, mask, lam):
    s = jnp.einsum('bqhmd,bkhmd->bhmqk', q, k).astype(jnp.float32) * (HEAD_DK ** -0.5)
    s = jnp.where(mask, s, NEG)
    p = jax.nn.softmax(s, axis=-1)
    a = p[:, :, 0] - lam * p[:, :, 1]
    return jnp.einsum('bhqk,bkhd->bqhd', a.astype(v.dtype), v)


def _prompt_attend(q, k, v, lam):
    b, s = q.shape[0], q.shape[1]
    nqb = s // Q_BLOCK
    qb = q.reshape(b, nqb, Q_BLOCK, N_HEADS, 2, HEAD_DK).transpose(1, 0, 2, 3, 4, 5)
    kpos = jnp.arange(s)

    def blk(args):
        qi, bi = args
        qpos = bi * Q_BLOCK + jnp.arange(Q_BLOCK)
        return _diff_attend(qi, k, v, kpos[None, :] <= qpos[:, None], lam)

    o = lax.map(blk, (qb, jnp.arange(nqb)))
    return o.transpose(1, 0, 2, 3, 4).reshape(b, s, N_HEADS, HEAD_DV)


def _even_mixer(xn, w_in, w_out, conv_w, qg, kg, lq1, lk1, lq2, lk2, sub_g, lam_init, pos0, conv_prev, past):
    b, t, _ = xn.shape
    h = xn @ w_in
    u, g_in, g_out, q, k, v = jnp.split(
        h, [D_CONV, 2 * D_CONV, 3 * D_CONV, 3 * D_CONV + D_QK, 3 * D_CONV + 2 * D_QK], axis=-1)
    conv, conv_new = _causal_dwconv(g_in * u, conv_w, conv_prev)
    y_a = g_out * conv
    pos = pos0 + jnp.arange(t, dtype=jnp.float32)
    q = _rope(_rms_norm(q.reshape(b, t, N_HEADS, 2, HEAD_DK), qg), pos)
    k = _rope(_rms_norm(k.reshape(b, t, N_HEADS, 2, HEAD_DK), kg), pos)
    v = v.reshape(b, t, N_HEADS, HEAD_DV)
    f32 = jnp.float32
    lam = (jnp.exp(jnp.sum(lq1.astype(f32) * lk1.astype(f32)))
           - jnp.exp(jnp.sum(lq2.astype(f32) * lk2.astype(f32))) + lam_init)
    if past is None:
        o = _prompt_attend(q, k, v, lam)
    else:
        pk, pv = past
        n_past = pk.shape[1]
        k_all = jnp.concatenate([pk.astype(k.dtype), k], axis=1)
        v_all = jnp.concatenate([pv.astype(v.dtype), v], axis=1)
        mask = jnp.arange(n_past + t)[None, :] <= (n_past + jnp.arange(t))[:, None]
        o = _diff_attend(q, k_all, v_all, mask, lam)
    o = _rms_norm(o, sub_g) * (1.0 - lam_init)
    y = jnp.concatenate([y_a, o.reshape(b, t, D_ATTN).astype(y_a.dtype)], axis=-1) @ w_out
    return y, k.reshape(b, t, N_HEADS, 2 * HEAD_DK), v, conv_new


def _rglru(x, wa, ba, wi, bi, lam_param, pos0, h0):
    b, t, _ = x.shape
    f32 = jnp.float32
    xf = x.astype(f32)
    xb = xf.reshape(b, t, RG_BLOCKS, RG_BW)
    r = jax.nn.sigmoid(jnp.einsum('btnc,ncd->btnd', xb, wa.astype(f32)).reshape(b, t, D_RNN) + ba.astype(f32))
    i = jax.nn.sigmoid(jnp.einsum('btnc,ncd->btnd', xb, wi.astype(f32)).reshape(b, t, D_RNN) + bi.astype(f32))
    log_a = -RG_C * r * jax.nn.softplus(-lam_param.astype(f32))
    a = jnp.exp(log_a)
    mult = jnp.sqrt(1.0 - jnp.exp(2.0 * log_a))
    first = (pos0 + jnp.arange(t)) == 0
    mult = jnp.where(first[None, :, None], 1.0, mult)
    bterm = mult * i * xf
    bterm = bterm.at[:, 0].add(a[:, 0] * h0.astype(f32))

    def comb(lhs, rhs):
        al, bl = lhs
        ar, br = rhs
        return al * ar, ar * bl + br

    _, h = lax.associative_scan(comb, (a, bterm), axis=1)
    return h, h[:, -1]


def _odd_mixer(xn, w_in, conv_w, conv_b, wa, ba, wi, bi, lam_param, w_out, pos0, conv_prev, h0):
    gy, xb = jnp.split(xn @ w_in, 2, axis=-1)
    gy = jax.nn.gelu(gy, approximate=True)
    xc, conv_new = _causal_dwconv(xb, conv_w, conv_prev)
    xc = xc + conv_b
    h, h_last = _rglru(xc, wa, ba, wi, bi, lam_param, pos0, h0)
    y = (h.astype(xn.dtype) * gy) @ w_out
    return y, conv_new, h_last.astype(h0.dtype)


def _hier_moe(xn, wg, bg, we, be, w1, w3, w2):
    b, t, d = xn.shape
    f32 = jnp.float32
    xt = xn.reshape(b * t, d)
    xf = xt.astype(f32)
    lg = xf @ wg.astype(f32) + bg.astype(f32)
    pg = jax.nn.softmax(lg, axis=-1)
    _, gsel = lax.top_k(lg, 1)
    w_grp = jnp.take_along_axis(pg, gsel, axis=1)
    le = (xf @ we.astype(f32) + be.astype(f32)).reshape(-1, N_GROUPS, E_PER_GROUP)
    le_sel = jnp.take_along_axis(le, gsel[:, :, None], axis=1)[:, 0]
    top_v, top_i = lax.top_k(le_sel, TOP_K_INNER)
    w_exp = jax.nn.softmax(top_v, axis=-1) * w_grp
    eid = gsel * E_PER_GROUP + top_i
    gates = jnp.sum(jax.nn.one_hot(eid, N_EXPERTS, dtype=f32) * w_exp[..., None], axis=1)
    hid = jax.nn.silu(jnp.einsum('nd,edf->nef', xt, w1)) * jnp.einsum('nd,edf->nef', xt, w3)
    out = jnp.einsum('nef,efd->nd', hid * gates[..., None].astype(hid.dtype), w2)
    return out.reshape(b, t, d).astype(xn.dtype)


def _trunk(x, p, pos0, sconv_prev, rg_conv_prev, rg_h0, cache_k, cache_v, page_table, W):
    new_k, new_v, new_sc, new_rc, new_rh = [], [], [], [], []
    for i in range(DEPTH):
        xn = _rms_norm(x, W['norm_mix'][i])
        if i % 2 == 0:
            j = i // 2
            if cache_k is None:
                past = None
            else:
                db, n_pages = page_table.shape
                pk = cache_k[j][page_table].reshape(db, n_pages * cache_k.shape[2], N_HEADS, 2, HEAD_DK)
                pv = cache_v[j][page_table].reshape(db, n_pages * cache_v.shape[2], N_HEADS, HEAD_DV)
                past = (pk, pv)
            lam_init = 0.8 - 0.6 * math.exp(-0.3 * i)
            y, kn, vn, sc = _even_mixer(
                xn, W['w_in_even'][j], W['w_out_even'][j], W['sconv_w'][j], W['q_norm_g'][j],
                W['k_norm_g'][j], W['lam_q1'][j], W['lam_k1'][j], W['lam_q2'][j], W['lam_k2'][j],
                W['subln_g'][j], lam_init, pos0, sconv_prev[j], past)
            new_k.append(kn)
            new_v.append(vn)
            new_sc.append(sc)
        else:
            j = i // 2
            y, rc, rh = _odd_mixer(
                xn, W['w_in_odd'][j], W['rg_conv_w'][j], W['rg_conv_b'][j], W['rg_wa'][j], W['rg_ba'][j],
                W['rg_wi'][j], W['rg_bi'][j], W['rg_lambda'][j], W['w_out_odd'][j], pos0,
                rg_conv_prev[j], rg_h0[j])
            new_rc.append(rc)
            new_rh.append(rh)
        x = x + y.astype(x.dtype)
        x = x + _hier_moe(_rms_norm(x, W['norm_ffn'][i]), W['moe_wg'][i], W['moe_bg'][i], W['moe_we'][i],
                          W['moe_be'][i], W['moe_w1'][i], W['moe_w3'][i], W['moe_w2'][i])
        gate = jax.nn.sigmoid((_rms_norm(x, W['norm_ple'][i]) @ W['ple_gate'][i]).astype(jnp.float32))
        x = x + (gate * (p[i] @ W['ple_proj'][i]).astype(jnp.float32)).astype(x.dtype)
    return (x, jnp.stack(new_k), jnp.stack(new_v), jnp.stack(new_sc), jnp.stack(new_rc), jnp.stack(new_rh))


def setup_inputs(seed: int = 0) -> dict:
    key = jax.random.key(seed)
    ks = jax.random.split(key, 64)
    cnt = [0]
    f32 = jnp.float32

    def nk():
        cnt[0] += 1
        return ks[cnt[0] - 1]

    def nrm(shape, scale):
        return jax.random.normal(nk(), shape, f32) * scale

    def gain(shape):
        return 1.0 + nrm(shape, 0.02)

    n_pages = PAST_LEN // PAGE_SIZE
    n_used = DEC_BATCH * n_pages
    n_pool = n_used + (n_used + 3) // 4
    page_table = jax.random.permutation(nk(), n_pool)[:n_used].reshape(DEC_BATCH, n_pages).astype(jnp.int32)
    u = jax.random.uniform(nk(), (N_ODD, D_RNN), f32, 0.9, 0.999)
    s = u ** (1.0 / RG_C)
    rg_lambda = jnp.log(s) - jnp.log1p(-s)
    d = D_MODEL
    out = {}
    out['x_prompt'] = nrm((BATCH, SEQ, d), 1.0)
    out['x_sample'] = nrm((DEC_BATCH, DEC_SEQ, d), 1.0)
    out['cache_k'] = nrm((N_EVEN, n_pool, PAGE_SIZE, N_HEADS, 2 * HEAD_DK), 1.0)
    out['cache_v'] = nrm((N_EVEN, n_pool, PAGE_SIZE, N_HEADS, HEAD_DV), 1.0)
    out['state_sconv'] = nrm((N_EVEN, DEC_BATCH, SCONV_W - 1, D_CONV), 1.0)
    out['state_rg_conv'] = nrm((N_ODD, DEC_BATCH, RG_CONV_W - 1, D_RNN), 1.0)
    out['state_rg_h'] = nrm((N_ODD, DEC_BATCH, D_RNN), 1.0)
    out['page_table'] = page_table
    out['p_prompt'] = nrm((DEPTH, BATCH, SEQ, D_PLE), 1.0)
    out['p_sample'] = nrm((DEPTH, DEC_BATCH, DEC_SEQ, D_PLE), 1.0)
    out['norm_mix'] = gain((DEPTH, d))
    out['norm_ffn'] = gain((DEPTH, d))
    out['norm_ple'] = gain((DEPTH, d))
    out['w_in_even'] = nrm((N_EVEN, d, D_IN_EVEN), d ** -0.5)
    out['w_out_even'] = nrm((N_EVEN, D_CONV + D_ATTN, d), (D_CONV + D_ATTN) ** -0.5)
    out['sconv_w'] = nrm((N_EVEN, SCONV_W, D_CONV), SCONV_W ** -0.5)
    out['q_norm_g'] = gain((N_EVEN, HEAD_DK))
    out['k_norm_g'] = gain((N_EVEN, HEAD_DK))
    out['lam_q1'] = nrm((N_EVEN, HEAD_DK), 0.1)
    out['lam_k1'] = nrm((N_EVEN, HEAD_DK), 0.1)
    out['lam_q2'] = nrm((N_EVEN, HEAD_DK), 0.1)
    out['lam_k2'] = nrm((N_EVEN, HEAD_DK), 0.1)
    out['subln_g'] = gain((N_EVEN, HEAD_DV))
    out['w_in_odd'] = nrm((N_ODD, d, 2 * D_RNN), d ** -0.5)
    out['rg_conv_w'] = nrm((N_ODD, RG_CONV_W, D_RNN), RG_CONV_W ** -0.5)
    out['rg_conv_b'] = nrm((N_ODD, D_RNN), 0.01)
    out['rg_wa'] = nrm((N_ODD, RG_BLOCKS, RG_BW, RG_BW), RG_BW ** -0.5)
    out['rg_ba'] = nrm((N_ODD, D_RNN), 0.01)
    out['rg_wi'] = nrm((N_ODD, RG_BLOCKS, RG_BW, RG_BW), RG_BW ** -0.5)
    out['rg_bi'] = nrm((N_ODD, D_RNN), 0.01)
    out['rg_lambda'] = rg_lambda
    out['w_out_odd'] = nrm((N_ODD, D_RNN, d), D_RNN ** -0.5)
    out['moe_wg'] = nrm((DEPTH, d, N_GROUPS), d ** -0.5)
    out['moe_bg'] = nrm((DEPTH, N_GROUPS), 0.01)
    out['moe_we'] = nrm((DEPTH, d, N_EXPERTS), d ** -0.5)
    out['moe_be'] = nrm((DEPTH, N_EXPERTS), 0.01)
    out['moe_w1'] = nrm((DEPTH, N_EXPERTS, d, D_EXPERT), d ** -0.5)
    out['moe_w3'] = nrm((DEPTH, N_EXPERTS, d, D_EXPERT), d ** -0.5)
    out['moe_w2'] = nrm((DEPTH, N_EXPERTS, D_EXPERT, d), D_EXPERT ** -0.5)
    out['ple_proj'] = nrm((DEPTH, D_PLE, d), D_PLE ** -0.5)
    out['ple_gate'] = nrm((DEPTH, d, d), d ** -0.5)
    return out


def reference(x_prompt, x_sample, cache_k, cache_v, state_sconv, state_rg_conv, state_rg_h, page_table,
              p_prompt, p_sample, norm_mix, norm_ffn, norm_ple, w_in_even, w_out_even, sconv_w, q_norm_g,
              k_norm_g, lam_q1, lam_k1, lam_q2, lam_k2, subln_g, w_in_odd, rg_conv_w, rg_conv_b, rg_wa, rg_ba,
              rg_wi, rg_bi, rg_lambda, w_out_odd, moe_wg, moe_bg, moe_we, moe_be, moe_w1, moe_w3, moe_w2,
              ple_proj, ple_gate):
    W = dict(norm_mix=norm_mix, norm_ffn=norm_ffn, norm_ple=norm_ple, w_in_even=w_in_even,
             w_out_even=w_out_even, sconv_w=sconv_w, q_norm_g=q_norm_g, k_norm_g=k_norm_g,
             lam_q1=lam_q1, lam_k1=lam_k1, lam_q2=lam_q2, lam_k2=lam_k2, subln_g=subln_g,
             w_in_odd=w_in_odd, rg_conv_w=rg_conv_w, rg_conv_b=rg_conv_b, rg_wa=rg_wa, rg_ba=rg_ba,
             rg_wi=rg_wi, rg_bi=rg_bi, rg_lambda=rg_lambda, w_out_odd=w_out_odd, moe_wg=moe_wg,
             moe_bg=moe_bg, moe_we=moe_we, moe_be=moe_be, moe_w1=moe_w1, moe_w3=moe_w3, moe_w2=moe_w2,
             ple_proj=ple_proj, ple_gate=ple_gate)
    b = x_prompt.shape[0]
    dt = x_prompt.dtype
    sc0 = jnp.zeros((N_EVEN, b, SCONV_W - 1, D_CONV), dt)
    rc0 = jnp.zeros((N_ODD, b, RG_CONV_W - 1, D_RNN), dt)
    rh0 = jnp.zeros((N_ODD, b, D_RNN), dt)
    y_prompt, k_p, v_p, sc_p, rc_p, rh_p = _trunk(
        x_prompt, p_prompt, 0, sc0, rc0, rh0, None, None, None, W)
    past_len = page_table.shape[1] * cache_k.shape[2]
    y_sample, k_s, v_s, sc_s, rc_s, rh_s = _trunk(
        x_sample, p_sample, past_len, state_sconv, state_rg_conv, state_rg_h, cache_k, cache_v, page_table, W)
    return (y_prompt, y_sample, k_p, v_p, sc_p, rc_p, rh_p, k_s, v_s, sc_s, rc_s, rh_s)
```

```python
import functools
import math

import jax
import jax.numpy as jnp
from jax import lax
from jax.experimental import pallas as pl
from jax.experimental.pallas import tpu as pltpu

F32 = jnp.float32
BF16 = jnp.bfloat16
I32 = jnp.int32
HIGHEST = lax.Precision.HIGHEST

N_HEADS = 8
HEAD_DK = 64
HEAD_DV = 128
ROPE_THETA = 10000.0
RG_C = 8.0
N_GROUPS = 4
E_PER_GROUP = 8
N_EXPERTS = N_GROUPS * E_PER_GROUP
EPS = 1e-6
NEG = -1e30

LANES = 128
SUBLANES = 8
ROW_PAD = SUBLANES
VMEM_LIMIT = 56 * 1024 * 1024

MOE_TM = 128
ATT_TK = 256


def _cparams(sem):
    return pltpu.CompilerParams(dimension_semantics=sem, vmem_limit_bytes=VMEM_LIMIT)


def _pick(n, pref):
    t = min(n, pref)
    while n % t:
        t //= 2
    return t


def _act_dtype(precise):
    return F32 if precise else BF16


def _mm(a, w_ref, precise):
    if precise:
        return jnp.dot(a, w_ref[...], preferred_element_type=F32, precision=HIGHEST)
    return jnp.dot(a, w_ref[...].astype(BF16), preferred_element_type=F32)


def _rms(x, g):
    ms = jnp.mean(x * x, axis=-1, keepdims=True)
    return x * lax.rsqrt(ms + EPS) * g


def _rms_mm_body(x_ref, g_ref, w_ref, o_ref, xn_ref, *, precise):
    @pl.when(pl.program_id(1) == 0)
    def _():
        xn_ref[...] = _rms(x_ref[...], g_ref[...]).astype(xn_ref.dtype)

    o_ref[...] = _mm(xn_ref[...], w_ref, precise)


def rms_matmul(x, g, w, layer, precise, tm_pref=1024, tn=512):
    m, d = x.shape
    n = w.shape[-1]
    tm = _pick(m, tm_pref)
    return pl.pallas_call(
        functools.partial(_rms_mm_body, precise=precise),
        grid=(m // tm, n // tn),
        in_specs=[
            pl.BlockSpec((tm, d), lambda i, j: (i, 0)),
            pl.BlockSpec((1, d), lambda i, j: (0, 0)),
            pl.BlockSpec((None, d, tn), lambda i, j: (layer, 0, j)),
        ],
        out_specs=pl.BlockSpec((tm, tn), lambda i, j: (i, j)),
        out_shape=jax.ShapeDtypeStruct((m, n), F32),
        scratch_shapes=[pltpu.VMEM((tm, d), _act_dtype(precise))],
        compiler_params=_cparams(("parallel", "arbitrary")),
        name="rms_matmul",
    )(x, g.reshape(1, d), w)


def _mm_res_body(*refs, nparts, precise):
    a_refs = refs[:nparts]
    w_refs = refs[nparts:2 * nparts]
    r_ref = refs[2 * nparts]
    o_ref = refs[2 * nparts + 1]
    acc = r_ref[...]
    for a, w in zip(a_refs, w_refs):
        acc = acc + _mm(a[...], w, precise)
    o_ref[...] = acc


def matmul_residual(parts, w, layer, resid, precise, tm_pref=1024, tn=512):
    m, n = resid.shape
    kp = parts[0].shape[1]
    nparts = len(parts)
    tm = _pick(m, tm_pref)
    in_specs = [pl.BlockSpec((tm, kp), lambda i, j: (i, 0)) for _ in parts]
    in_specs += [pl.BlockSpec((None, kp, tn), functools.partial(lambda i, j, p: (layer, p, j), p=p))
                 for p in range(nparts)]
    in_specs += [pl.BlockSpec((tm, tn), lambda i, j: (i, j))]
    return pl.pallas_call(
        functools.partial(_mm_res_body, nparts=nparts, precise=precise),
        grid=(m // tm, n // tn),
        in_specs=in_specs,
        out_specs=pl.BlockSpec((tm, tn), lambda i, j: (i, j)),
        out_shape=jax.ShapeDtypeStruct((m, n), F32),
        compiler_params=_cparams(("parallel", "arbitrary")),
        name="matmul_residual",
    )(*parts, *([w] * nparts), resid)


def _even_prep_body(pos0_ref, u_ref, gi_ref, go_ref, q_ref, k_ref, v_ref, cw_ref, prev_ref, qg_ref, kg_ref,
                    ones_ref, inv_ref,
                    ya_ref, tail_ref, qb_ref, ko_ref, kb_ref, vo_ref, vb_ref, carry_ref, *, tt, tail_rows):
    b = pl.program_id(0)
    t = pl.program_id(1)

    @pl.when(t == 0)
    def _():
        carry_ref[...] = prev_ref[0]

    z = gi_ref[0] * u_ref[0]
    c = z.shape[1]
    row = lax.broadcasted_iota(I32, z.shape, 0)
    c6 = carry_ref[6:7, :]
    c7 = carry_ref[7:8, :]
    z1 = jnp.where(row == 0, c7, pltpu.roll(z, 1, 0))
    z2 = jnp.where(row == 0, c6, jnp.where(row == 1, c7, pltpu.roll(z, 2, 0)))
    conv = z2 * cw_ref[0:1, :] + z1 * cw_ref[1:2, :] + z * cw_ref[2:3, :]
    ya_ref[0] = (go_ref[0] * conv).astype(ya_ref.dtype)
    carry_ref[...] = z[tt - SUBLANES:tt, :]
    tail_ref[0] = z[tt - tail_rows:tt, :]

    pos = (pos0_ref[b] + t * tt + lax.broadcasted_iota(I32, (tt, LANES), 0)).astype(F32)
    ang = pos * inv_ref[...]
    lane = lax.broadcasted_iota(I32, (tt, LANES), 1)
    first_half = (lane & (HEAD_DK // 2)) == 0
    cos1 = jnp.cos(ang)
    sin1 = jnp.sin(ang)
    sin1 = jnp.where(first_half, -sin1, sin1)
    reps = c // LANES
    cos = jnp.concatenate([cos1] * reps, axis=1)
    sin = jnp.concatenate([sin1] * reps, axis=1)
    first_half_c = jnp.concatenate([first_half] * reps, axis=1)
    ones = ones_ref[...]

    def norm_rope(x, gain):
        x2 = x * x
        hi = x2.astype(BF16)
        lo = (x2 - hi.astype(F32)).astype(BF16)
        ss = (jnp.dot(hi, ones, preferred_element_type=F32)
              + jnp.dot(lo, ones, preferred_element_type=F32))
        xn = x * lax.rsqrt(ss * (1.0 / HEAD_DK) + EPS) * gain
        half = HEAD_DK // 2
        swapped = jnp.where(first_half_c, pltpu.roll(xn, c - half, 1), pltpu.roll(xn, half, 1))
        return xn * cos + swapped * sin

    qr = norm_rope(q_ref[0], qg_ref[...])
    qb_ref[0] = (qr * (HEAD_DK ** -0.5)).astype(qb_ref.dtype)
    kr = norm_rope(k_ref[0], kg_ref[...])
    ko_ref[0] = kr
    kb_ref[0] = kr.astype(BF16)
    v = v_ref[0]
    vo_ref[0] = v
    vb_ref[0] = v.astype(BF16)


def even_prep(h, conv_w, prev, qg, kg, pos0, precise, tt_pref=256):
    b, t, c6 = h.shape
    c = c6 // 6
    tt = _pick(t, tt_pref)
    tail_rows = min(2 * SUBLANES, tt)
    cw = jnp.zeros((SUBLANES, c), F32).at[:conv_w.shape[0]].set(conv_w)
    grp = jnp.arange(c) // HEAD_DK
    ones_bd = (grp[:, None] == grp[None, :]).astype(BF16)
    half = HEAD_DK // 2
    inv = jnp.exp(-math.log(ROPE_THETA) * jnp.arange(half, dtype=F32) * 2.0 / HEAD_DK)
    inv_row = jnp.tile(inv, LANES // half).reshape(1, LANES)
    qg_row = jnp.tile(qg, c // HEAD_DK).reshape(1, c)
    kg_row = jnp.tile(kg, c // HEAD_DK).reshape(1, c)

    def col(j):
        return pl.BlockSpec((1, tt, c), lambda bi, ti, p0: (bi, ti, j))

    const = lambda shape: pl.BlockSpec(shape, lambda bi, ti, p0: (0,) * len(shape))
    tile_spec = pl.BlockSpec((1, tt, c), lambda bi, ti, p0: (bi, ti, 0))
    prev_spec = pl.BlockSpec((1, SUBLANES, c), lambda bi, ti, p0: (bi, 0, 0))
    tail_spec = pl.BlockSpec((1, tail_rows, c), lambda bi, ti, p0: (bi, 0, 0))
    act = _act_dtype(precise)
    grid_spec = pltpu.PrefetchScalarGridSpec(
        num_scalar_prefetch=1,
        grid=(b, t // tt),
        in_specs=[col(0), col(1), col(2), col(3), col(4), col(5),
                  const((SUBLANES, c)), prev_spec, const((1, c)), const((1, c)),
                  const((c, c)), const((1, LANES))],
        out_specs=[tile_spec, tail_spec, tile_spec, tile_spec, tile_spec, tile_spec, tile_spec],
        scratch_shapes=[pltpu.VMEM((SUBLANES, c), F32)],
    )
    return pl.pallas_call(
        functools.partial(_even_prep_body, tt=tt, tail_rows=tail_rows),
        grid_spec=grid_spec,
        out_shape=[
            jax.ShapeDtypeStruct((b, t, c), act),
            jax.ShapeDtypeStruct((b, tail_rows, c), F32),
            jax.ShapeDtypeStruct((b, t, c), act),
            jax.ShapeDtypeStruct((b, t, c), F32),
            jax.ShapeDtypeStruct((b, t, c), BF16),
            jax.ShapeDtypeStruct((b, t, c), F32),
            jax.ShapeDtypeStruct((b, t, c), BF16),
        ],
        compiler_params=_cparams(("parallel", "arbitrary")),
        name="even_prep",
    )(pos0, h, h, h, h, h, h, cw, prev, qg_row, kg_row, ones_bd, inv_row)


def _lambda_value(lamv_ref, lam_init):
    lv = lamv_ref[...]
    s1 = jnp.sum(lv[0:1, :] * lv[1:2, :], axis=-1, keepdims=True)
    s2 = jnp.sum(lv[2:3, :] * lv[3:4, :], axis=-1, keepdims=True)
    return jnp.exp(s1) - jnp.exp(s2) + lam_init


def _sub_norm(o, subg, lam_init):
    return _rms(o, subg) * (1.0 - lam_init)


def _split_maps(q):
    lane = lax.broadcasted_iota(I32, q.shape, 1)
    zero = jnp.zeros_like(q)
    return jnp.concatenate([jnp.where(lane < HEAD_DK, q, zero), jnp.where(lane >= HEAD_DK, q, zero)],
                           axis=0).astype(BF16)


def _flash_body(lamv_ref, subg_ref, q_ref, k_ref, v_ref, o_ref, *, tq, tk, q_base, lam_init):
    q_pos0 = q_base + pl.program_id(2) * tq
    qq = _split_maps(q_ref[0])

    def step(start, carry, masked):
        m, l, acc = carry
        k = k_ref[0, pl.ds(start, tk), :]
        v = v_ref[0, pl.ds(start, tk), :]
        s = lax.dot_general(qq, k, (((1,), (1,)), ((), ())), preferred_element_type=F32)
        if masked:
            r = lax.broadcasted_iota(I32, (tq, tk), 0)
            cidx = lax.broadcasted_iota(I32, (tq, tk), 1)
            keep = (start + cidx) <= (q_pos0 + r)
            s = jnp.where(jnp.concatenate([keep, keep], axis=0), s, NEG)
        m_new = jnp.maximum(m, jnp.max(s, axis=-1, keepdims=True))
        alpha = jnp.exp(m - m_new)
        p = jnp.exp(s - m_new)
        l = alpha * l + jnp.sum(p, axis=-1, keepdims=True)
        acc = alpha * acc + jnp.dot(p.astype(BF16), v, preferred_element_type=F32)
        return m_new, l, acc

    init = (jnp.full((2 * tq, 1), NEG, F32), jnp.zeros((2 * tq, 1), F32), jnp.zeros((2 * tq, HEAD_DV), F32))
    n_full = q_pos0 // tk
    carry = lax.fori_loop(0, n_full, lambda ki, cr: step(pl.multiple_of(ki * tk, tk), cr, False), init)
    m, l, acc = step(pl.multiple_of(n_full * tk, tk), carry, True)
    lam = _lambda_value(lamv_ref, lam_init)
    o = acc[:tq] / l[:tq] - lam * (acc[tq:] / l[tq:])
    o_ref[0] = _sub_norm(o, subg_ref[...], lam_init).astype(o_ref.dtype)


def prompt_attention(q, kb, vb, lamv, subg, lam_init, q_base, precise, tq_pref=256):
    b, tql, c = q.shape
    t = kb.shape[1]
    tq = _pick(tql, tq_pref)
    tk = _pick(t, ATT_TK)
    assert (q_base + tql - 1) // tk == (q_base + tql - tq) // tk or tq == tk
    return pl.pallas_call(
        functools.partial(_flash_body, tq=tq, tk=tk, q_base=q_base, lam_init=lam_init),
        grid=(b, N_HEADS, tql // tq),
        in_specs=[
            pl.BlockSpec((4, HEAD_DK), lambda bi, h, qi: (0, 0)),
            pl.BlockSpec((1, HEAD_DV), lambda bi, h, qi: (0, 0)),
            pl.BlockSpec((1, tq, HEAD_DV), lambda bi, h, qi: (bi, qi, h)),
            pl.BlockSpec((1, t, HEAD_DV), lambda bi, h, qi: (bi, 0, h)),
            pl.BlockSpec((1, t, HEAD_DV), lambda bi, h, qi: (bi, 0, h)),
        ],
        out_specs=pl.BlockSpec((1, tq, HEAD_DV), lambda bi, h, qi: (bi, qi, h)),
        out_shape=jax.ShapeDtypeStruct((b, tql, c), _act_dtype(precise)),
        compiler_params=_cparams(("parallel", "parallel", "arbitrary")),
        name="prompt_attention",
    )(lamv, subg.reshape(1, HEAD_DV), q, kb, vb)


def _decode_body(pt_ref, lamv_ref, subg_ref, q_ref, kn_ref, vn_ref, *refs, npg, t_valid, lam_init):
    k_refs = refs[:npg]
    v_refs = refs[npg:2 * npg]
    o_ref = refs[2 * npg]
    qs_ref, m_ref, l_ref, acc_ref = refs[2 * npg + 1:]
    step_i = pl.program_id(1)
    rows = 2 * SUBLANES

    @pl.when(step_i == 0)
    def _():
        q = q_ref[0]
        lane = lax.broadcasted_iota(I32, (SUBLANES, HEAD_DV), 1)
        for h in range(N_HEADS):
            qh = q[:, h * HEAD_DV:(h + 1) * HEAD_DV]
            qs_ref[h * rows:h * rows + SUBLANES, :] = jnp.where(lane < HEAD_DK, qh, 0.0)
            qs_ref[h * rows + SUBLANES:(h + 1) * rows, :] = jnp.where(lane >= HEAD_DK, qh, 0.0)
        m_ref[...] = jnp.full(m_ref.shape, NEG, F32)
        l_ref[...] = jnp.zeros(l_ref.shape, F32)
        acc_ref[...] = jnp.zeros(acc_ref.shape, F32)

    def update(h, s, vh):
        sl = slice(h * rows, (h + 1) * rows)
        m_prev = m_ref[sl, :]
        m_new = jnp.maximum(m_prev, jnp.max(s, axis=-1, keepdims=True))
        alpha = jnp.exp(m_prev - m_new)
        p = jnp.exp(s - m_new[:, 0:1])
        l_ref[sl, :] = alpha * l_ref[sl, :] + jnp.sum(p, axis=-1, keepdims=True)
        acc_ref[sl, :] = alpha * acc_ref[sl, :] + jnp.dot(p.astype(BF16), vh, preferred_element_type=F32)
        m_ref[sl, :] = m_new

    page = k_refs[0].shape[0] // N_HEADS
    for h in range(N_HEADS):
        kh = jnp.concatenate([r[pl.ds(h, page, stride=N_HEADS), :] for r in k_refs], axis=0).astype(BF16)
        vh = jnp.concatenate([r[pl.ds(h, page, stride=N_HEADS), :] for r in v_refs], axis=0).astype(BF16)
        qh = qs_ref[h * rows:(h + 1) * rows, :].astype(BF16)
        s = lax.dot_general(qh, kh, (((1,), (1,)), ((), ())), preferred_element_type=F32)
        update(h, s, vh)

    @pl.when(step_i == pl.num_programs(1) - 1)
    def _():
        kn = kn_ref[0]
        vn = vn_ref[0]
        r = lax.broadcasted_iota(I32, (rows, SUBLANES), 0) % SUBLANES
        cidx = lax.broadcasted_iota(I32, (rows, SUBLANES), 1)
        keep = (cidx <= r) & (cidx < t_valid)
        lam = _lambda_value(lamv_ref, lam_init)
        for h in range(N_HEADS):
            cs = slice(h * HEAD_DV, (h + 1) * HEAD_DV)
            qh = qs_ref[h * rows:(h + 1) * rows, :].astype(BF16)
            s = lax.dot_general(qh, kn[:, cs], (((1,), (1,)), ((), ())), preferred_element_type=F32)
            update(h, jnp.where(keep, s, NEG), vn[:, cs])
            a = acc_ref[h * rows:(h + 1) * rows, :] / l_ref[h * rows:(h + 1) * rows, :]
            o = a[:SUBLANES] - lam * a[SUBLANES:]
            o_ref[0, :, cs] = _sub_norm(o, subg_ref[...], lam_init)


def sample_attention(q, kb, vb, cache_k, cache_v, layer, page_table, lamv, subg, lam_init, t_valid, npg=4):
    bs, n_pages = page_table.shape
    c = q.shape[-1]
    pr = cache_k.shape[2]
    npg = _pick(n_pages, npg)
    n_steps = n_pages // npg

    def page_spec(i):
        return pl.BlockSpec((None, None, pr, HEAD_DV),
                            lambda bi, si, pt: (layer, pt[bi, si * npg + i], 0, 0))

    row_spec = pl.BlockSpec((1, ROW_PAD, c), lambda bi, si, pt: (bi, 0, 0))
    grid_spec = pltpu.PrefetchScalarGridSpec(
        num_scalar_prefetch=1,
        grid=(bs, n_steps),
        in_specs=[pl.BlockSpec((4, HEAD_DK), lambda bi, si, pt: (0, 0)),
                  pl.BlockSpec((1, HEAD_DV), lambda bi, si, pt: (0, 0)),
                  row_spec, row_spec, row_spec]
                 + [page_spec(i) for i in range(npg)] + [page_spec(i) for i in range(npg)],
        out_specs=row_spec,
        scratch_shapes=[pltpu.VMEM((N_HEADS * 2 * SUBLANES, HEAD_DV), F32) for _ in range(4)],
    )
    return pl.pallas_call(
        functools.partial(_decode_body, npg=npg, t_valid=t_valid, lam_init=lam_init),
        grid_spec=grid_spec,
        out_shape=jax.ShapeDtypeStruct((bs, ROW_PAD, c), F32),
        compiler_params=_cparams(("parallel", "arbitrary")),
        name="sample_attention",
    )(page_table, lamv, subg.reshape(1, HEAD_DV), q, kb, vb, *([cache_k] * npg), *([cache_v] * npg))


def _rglru_body(pos0_ref, gy_ref, xb_ref, cw_ref, cb_ref, wa_ref, ba_ref, wi_ref, bi_ref, lam_ref,
                prevx_ref, prevh_ref, hy_ref, xtail_ref, htail_ref, xcarry, hcarry, wab, wib,
                *, tt, tail_rows, width, precise):
    b = pl.program_id(0)
    t = pl.program_id(1)

    @pl.when((b == 0) & (t == 0))
    def _():
        wab[...] = wa_ref[...].astype(wab.dtype)
        wib[...] = wi_ref[...].astype(wib.dtype)

    @pl.when(t == 0)
    def _():
        xcarry[...] = prevx_ref[0]
        hcarry[...] = prevh_ref[0]

    xb = xb_ref[0]
    row = lax.broadcasted_iota(I32, xb.shape, 0)

    def shifted(k):
        r = pltpu.roll(xb, k, 0)
        for i in range(k):
            r = jnp.where(row == i, xcarry[SUBLANES - k + i:SUBLANES - k + i + 1, :], r)
        return r

    xc = shifted(width - 1) * cw_ref[0:1, :]
    for j in range(1, width):
        xj = shifted(width - 1 - j) if j < width - 1 else xb
        xc = xc + xj * cw_ref[j:j + 1, :]
    xc = xc + cb_ref[...]

    nblk, bw, _ = wab.shape
    xcb = xc.astype(wab.dtype)
    prec = HIGHEST if precise else None

    def block_diag(w):
        return jnp.concatenate([jnp.dot(xcb[:, n * bw:(n + 1) * bw], w[n], preferred_element_type=F32,
                                        precision=prec) for n in range(nblk)], axis=1)

    r = jax.nn.sigmoid(block_diag(wab) + ba_ref[...])
    i = jax.nn.sigmoid(block_diag(wib) + bi_ref[...])
    nl = -lam_ref[...]
    softplus = jnp.maximum(nl, 0.0) + jnp.log(1.0 + jnp.exp(-jnp.abs(nl)))
    log_a = (-RG_C * r) * softplus
    a = jnp.exp(log_a)
    mult = jnp.sqrt(1.0 - jnp.exp(2.0 * log_a))
    pos = pos0_ref[b] + t * tt + row
    mult = jnp.where(pos == 0, 1.0, mult)
    bt = (mult * i) * xc

    dist = 1
    while dist < tt:
        valid = row >= dist
        a_s = jnp.where(valid, pltpu.roll(a, dist, 0), 1.0)
        b_s = jnp.where(valid, pltpu.roll(bt, dist, 0), 0.0)
        bt = a * b_s + bt
        a = a * a_s
        dist *= 2
    h = a * hcarry[SUBLANES - 1:SUBLANES, :] + bt

    hcarry[...] = h[tt - SUBLANES:tt, :]
    htail_ref[0] = h[tt - tail_rows:tt, :]
    xcarry[...] = xb[tt - SUBLANES:tt, :]
    xtail_ref[0] = xb[tt - tail_rows:tt, :]

    gy = gy_ref[0]
    gelu = 0.5 * gy * (1.0 + jnp.tanh(math.sqrt(2.0 / math.pi) * (gy + 0.044715 * (gy * gy * gy))))
    hy_ref[0] = (h * gelu).astype(hy_ref.dtype)


def rglru(h, conv_w, conv_b, wa, ba, wi, bi, lam, prevx, prevh, pos0, precise, tt_pref=128):
    b, t, d2 = h.shape
    d = d2 // 2
    tt = _pick(t, tt_pref)
    tail_rows = min(2 * SUBLANES, tt)
    width = conv_w.shape[0]
    cw = jnp.zeros((SUBLANES, d), F32).at[:width].set(conv_w)
    nblk, bw, _ = wa.shape
    const = lambda shape: pl.BlockSpec(shape, lambda bi_, ti, p0: (0,) * len(shape))
    prev_spec = pl.BlockSpec((1, SUBLANES, d), lambda bi_, ti, p0: (bi_, 0, 0))
    tail_spec = pl.BlockSpec((1, tail_rows, d), lambda bi_, ti, p0: (bi_, 0, 0))
    act = _act_dtype(precise)
    grid_spec = pltpu.PrefetchScalarGridSpec(
        num_scalar_prefetch=1,
        grid=(b, t // tt),
        in_specs=[pl.BlockSpec((1, tt, d), lambda bi_, ti, p0: (bi_, ti, 0)),
                  pl.BlockSpec((1, tt, d), lambda bi_, ti, p0: (bi_, ti, 1)),
                  const((SUBLANES, d)), const((1, d)),
                  const((nblk, bw, bw)), const((1, d)), const((nblk, bw, bw)), const((1, d)), const((1, d)),
                  prev_spec, prev_spec],
        out_specs=[pl.BlockSpec((1, tt, d), lambda bi_, ti, p0: (bi_, ti, 0)), tail_spec, tail_spec],
        scratch_shapes=[pltpu.VMEM((SUBLANES, d), F32), pltpu.VMEM((SUBLANES, d), F32),
                        pltpu.VMEM((nblk, bw, bw), act), pltpu.VMEM((nblk, bw, bw), act)],
    )
    return pl.pallas_call(
        functools.partial(_rglru_body, tt=tt, tail_rows=tail_rows, width=width, precise=precise),
        grid_spec=grid_spec,
        out_shape=[jax.ShapeDtypeStruct((b, t, d), act),
                   jax.ShapeDtypeStruct((b, tail_rows, d), F32),
                   jax.ShapeDtypeStruct((b, tail_rows, d), F32)],
        compiler_params=_cparams(("arbitrary", "arbitrary")),
        name="rglru",
    )(pos0, h, h, cw, conv_b.reshape(1, d), wa, ba.reshape(1, d), wi, bi.reshape(1, d), lam.reshape(1, d),
      prevx, prevh)


def _router_body(x_ref, g_ref, w_ref, b_ref, xn_ref, ids_ref, wts_ref):
    xn = _rms(x_ref[...], g_ref[...])
    xn_ref[...] = xn.astype(BF16)
    lg = jnp.dot(xn, w_ref[...], preferred_element_type=F32, precision=HIGHEST) + b_ref[...]
    lane = lax.broadcasted_iota(I32, lg.shape, 1)
    big = jnp.int32(LANES)

    def first_argmax(vals, mask):
        v = jnp.where(mask, vals, -jnp.inf)
        mx = jnp.max(v, axis=-1, keepdims=True)
        idx = jnp.min(jnp.where(mask & (v == mx), lane, big), axis=-1, keepdims=True)
        return mx, idx

    gmask = lane < N_GROUPS
    gmax, gsel = first_argmax(lg, gmask)
    gsum = jnp.sum(jnp.where(gmask, jnp.exp(lg - gmax), 0.0), axis=-1, keepdims=True)
    w_grp = 1.0 / gsum
    lo = N_GROUPS + gsel * E_PER_GROUP
    emask = (lane >= lo) & (lane < lo + E_PER_GROUP)
    v0, i0 = first_argmax(lg, emask)
    v1, i1 = first_argmax(lg, emask & (lane != i0))
    e1 = jnp.exp(v1 - v0)
    den = 1.0 + e1
    w0 = (1.0 / den) * w_grp
    w1 = (e1 / den) * w_grp
    ids_ref[...] = jnp.where(lane == 0, i0 - N_GROUPS, jnp.where(lane == 1, i1 - N_GROUPS, 0))
    wts_ref[...] = jnp.where(lane == 0, w0, jnp.where(lane == 1, w1, 0.0))


def router(x, g, wcat, bcat, tm_pref=512):
    m, d = x.shape
    tm = _pick(m, tm_pref)
    row = pl.BlockSpec((tm, LANES), lambda i: (i, 0))
    return pl.pallas_call(
        _router_body,
        grid=(m // tm,),
        in_specs=[pl.BlockSpec((tm, d), lambda i: (i, 0)), pl.BlockSpec((1, d), lambda i: (0, 0)),
                  pl.BlockSpec((d, LANES), lambda i: (0, 0)), pl.BlockSpec((1, LANES), lambda i: (0, 0))],
        out_specs=[pl.BlockSpec((tm, d), lambda i: (i, 0)), row, row],
        out_shape=[jax.ShapeDtypeStruct((m, d), BF16), jax.ShapeDtypeStruct((m, LANES), I32),
                   jax.ShapeDtypeStruct((m, LANES), F32)],
        compiler_params=_cparams(("parallel",)),
        name="router",
    )(x, g.reshape(1, d), wcat, bcat)


def _moe_body(te_ref, na_ref, x_ref, gate_ref, w1_ref, w3_ref, w2_ref, o_ref, w1b, w3b, w2b):
    t = pl.program_id(0)
    prev = te_ref[jnp.maximum(t - 1, 0)]

    @pl.when((t == 0) | (te_ref[t] != prev))
    def _():
        w1b[...] = w1_ref[...].astype(BF16)
        w3b[...] = w3_ref[...].astype(BF16)
        w2b[...] = w2_ref[...].astype(BF16)

    @pl.when(t < na_ref[0])
    def _():
        x = x_ref[...]
        h1 = jnp.dot(x, w1b[...], preferred_element_type=F32)
        h3 = jnp.dot(x, w3b[...], preferred_element_type=F32)
        hid = (h1 * jax.nn.sigmoid(h1)) * h3
        hid = (hid * gate_ref[...]).astype(BF16)
        o_ref[...] = jnp.dot(hid, w2b[...], preferred_element_type=F32)

    @pl.when(t >= na_ref[0])
    def _():
        o_ref[...] = jnp.zeros(o_ref.shape, F32)


def moe_grouped(xs, gate, tile_expert, n_active, w1, w3, w2, layer, tm):
    rows, d = xs.shape
    f = w1.shape[-1]
    nt = rows // tm
    grid_spec = pltpu.PrefetchScalarGridSpec(
        num_scalar_prefetch=2,
        grid=(nt,),
        in_specs=[pl.BlockSpec((tm, d), lambda t, te, na: (t, 0)),
                  pl.BlockSpec((tm, 1), lambda t, te, na: (t, 0)),
                  pl.BlockSpec((None, None, d, f), lambda t, te, na: (layer, te[t], 0, 0)),
                  pl.BlockSpec((None, None, d, f), lambda t, te, na: (layer, te[t], 0, 0)),
                  pl.BlockSpec((None, None, f, d), lambda t, te, na: (layer, te[t], 0, 0))],
        out_specs=pl.BlockSpec((tm, d), lambda t, te, na: (t, 0)),
        scratch_shapes=[pltpu.VMEM((d, f), BF16), pltpu.VMEM((d, f), BF16), pltpu.VMEM((f, d), BF16)],
    )
    return pl.pallas_call(
        _moe_body,
        grid_spec=grid_spec,
        out_shape=jax.ShapeDtypeStruct((rows, d), F32),
        compiler_params=_cparams(("arbitrary",)),
        name="moe_grouped",
    )(tile_expert, n_active, xs, gate, w1, w3, w2)


def _dispatch_plan(ids, wts, tm, nt):
    n = ids.shape[0]
    e = ids.reshape(-1)
    w = wts.reshape(-1)
    order = jnp.argsort(e, stable=True)
    es = e[order]
    counts = jnp.sum((e[:, None] == jnp.arange(N_EXPERTS, dtype=I32)[None, :]).astype(I32), axis=0)
    tiles = (counts + tm - 1) // tm
    tile_end = jnp.cumsum(tiles)
    pstart = (tile_end - tiles) * tm
    start = jnp.cumsum(counts) - counts
    rank = jnp.arange(2 * n, dtype=I32) - start[es]
    slot_sorted = pstart[es] + rank
    slot = jnp.zeros((2 * n,), I32).at[order].set(slot_sorted)
    src_tok = jnp.zeros((nt * tm,), I32).at[slot_sorted].set((order // 2).astype(I32))
    gate = jnp.zeros((nt * tm,), F32).at[slot_sorted].set(w[order])
    n_active = tile_end[-1]
    tix = jnp.minimum(jnp.arange(nt, dtype=I32), n_active - 1)
    tile_expert = jnp.minimum(jnp.searchsorted(tile_end, tix, side="right"), N_EXPERTS - 1).astype(I32)
    return slot.reshape(n, 2), src_tok, gate.reshape(-1, 1), tile_expert, n_active.reshape(1).astype(I32)


def _ple_body(x_ref, g_ref, wg_ref, p_ref, wp_ref, xc_ref, o_ref, xn_ref, *, precise):
    @pl.when(pl.program_id(1) == 0)
    def _():
        xn_ref[...] = _rms(x_ref[...], g_ref[...]).astype(xn_ref.dtype)

    gate = jax.nn.sigmoid(_mm(xn_ref[...], wg_ref, precise))
    proj = _mm(p_ref[...].astype(xn_ref.dtype), wp_ref, precise)
    o_ref[...] = xc_ref[...] + gate * proj


def ple(x, g, wg, p, wp, layer, precise, tm_pref=1024, tn=512):
    m, d = x.shape
    dp = p.shape[-1]
    tm = _pick(m, tm_pref)
    return pl.pallas_call(
        functools.partial(_ple_body, precise=precise),
        grid=(m // tm, d // tn),
        in_specs=[pl.BlockSpec((tm, d), lambda i, j: (i, 0)),
                  pl.BlockSpec((1, d), lambda i, j: (0, 0)),
                  pl.BlockSpec((None, d, tn), lambda i, j: (layer, 0, j)),
                  pl.BlockSpec((None, tm, dp), lambda i, j: (layer, i, 0)),
                  pl.BlockSpec((None, dp, tn), lambda i, j: (layer, 0, j)),
                  pl.BlockSpec((tm, tn), lambda i, j: (i, j))],
        out_specs=pl.BlockSpec((tm, tn), lambda i, j: (i, j)),
        out_shape=jax.ShapeDtypeStruct((m, d), F32),
        scratch_shapes=[pltpu.VMEM((tm, d), _act_dtype(precise))],
        compiler_params=_cparams(("parallel", "arbitrary")),
        name="ple",
    )(x, g.reshape(1, d), wg, p, wp, x)


def _pad_rows(a, axis, size, at_end=True):
    pad = [(0, 0)] * a.ndim
    pad[axis] = (0, size - a.shape[axis]) if at_end else (size - a.shape[axis], 0)
    return jnp.pad(a, pad)


def kernel(x_prompt, x_sample, cache_k, cache_v, state_sconv, state_rg_conv, state_rg_h, page_table, p_prompt, p_sample, norm_mix, norm_ffn, norm_ple, w_in_even, w_out_even, sconv_w, q_norm_g, k_norm_g, lam_q1, lam_k1, lam_q2, lam_k2, subln_g, w_in_odd, rg_conv_w, rg_conv_b, rg_wa, rg_ba, rg_wi, rg_bi, rg_lambda, w_out_odd, moe_wg, moe_bg, moe_we, moe_be, moe_w1, moe_w3, moe_w2, ple_proj, ple_gate):
    bp, tp, d = x_prompt.shape
    bs, ts, _ = x_sample.shape
    depth = norm_mix.shape[0]
    n_pool, page = cache_k.shape[1], cache_k.shape[2]
    past_len = page_table.shape[1] * page
    c = w_out_even.shape[1] // 2
    d_ple = p_prompt.shape[-1]
    r8 = ROW_PAD
    tail0 = tp - r8

    xb = x_prompt.reshape(bp * tp, d)
    xq = jnp.concatenate([_pad_rows(x_sample, 1, r8), x_prompt[:, tail0:]], axis=0).reshape((bs + bp) * r8, d)
    pb = p_prompt.reshape(depth, bp * tp, d_ple)
    pq = jnp.concatenate([_pad_rows(p_sample, 2, r8), p_prompt[:, :, tail0:]], axis=1).reshape(depth, -1, d_ple)
    n_b = xb.shape[0]
    bq = bs + bp
    pos0_b = jnp.zeros((bp,), I32)
    pos0_q = jnp.concatenate([jnp.full((bs,), past_len, I32), jnp.full((bp,), tail0, I32)])

    ck = cache_k.reshape(cache_k.shape[0], n_pool, page * N_HEADS, 2 * HEAD_DK)
    cv = cache_v.reshape(cache_v.shape[0], n_pool, page * N_HEADS, HEAD_DV)

    nt = -(-2 * (n_b + bq * r8) // MOE_TM) + N_EXPERTS

    def seq_rows(a, lo, hi):
        ends = [ts] * bs + [r8] * bp
        idx = jnp.asarray([s * r8 + e + k for s, e in enumerate(ends) for k in range(lo, hi)], I32)
        return jnp.take(a.reshape(bq * r8, a.shape[-1]), idx, axis=0).reshape(bq, hi - lo, a.shape[-1])

    new_k, new_v, new_sc, new_rc, new_rh = [], [], [], [], []
    for i in range(depth):
        j = i // 2
        if i % 2 == 0:
            lam_init = 0.8 - 0.6 * math.exp(-0.3 * i)
            lamv = jnp.stack([lam_q1[j], lam_k1[j], lam_q2[j], lam_k2[j]])
            h = rms_matmul(xb, norm_mix[i], w_in_even, j, False).reshape(bp, tp, 6 * c)
            ya, tail_b, qb, kf_b, kb, vf_b, vb = even_prep(
                h, sconv_w[j], jnp.zeros((bp, SUBLANES, c), F32), q_norm_g[j], k_norm_g[j], pos0_b, False)
            o = prompt_attention(qb, kb, vb, lamv, subln_g[j], lam_init, 0, False)
            xb = matmul_residual([ya.reshape(n_b, c), o.reshape(n_b, c)], w_out_even, j, xb, False)
            h = rms_matmul(xq, norm_mix[i], w_in_even, j, True).reshape(bq, r8, 6 * c)
            prev = jnp.concatenate([_pad_rows(state_sconv[j], 1, SUBLANES, at_end=False), tail_b[:, :SUBLANES]])
            ya, tail_q, qq, kf_q, kq, vf_q, vq = even_prep(
                h, sconv_w[j], prev, q_norm_g[j], k_norm_g[j], pos0_q, True)
            o_s = sample_attention(qq[:bs], kq[:bs], vq[:bs], ck, cv, j, page_table, lamv, subln_g[j],
                                   lam_init, ts)
            o_t = prompt_attention(qq[bs:], kb, vb, lamv, subln_g[j], lam_init, tail0, True)
            o = jnp.concatenate([o_s, o_t], axis=0)
            xq = matmul_residual([ya.reshape(bq * r8, c), o.reshape(bq * r8, c)], w_out_even, j, xq, True)
            new_k.append((kf_b, kf_q[:bs, :ts]))
            new_v.append((vf_b, vf_q[:bs, :ts]))
            new_sc.append(seq_rows(tail_q, -2, 0))
        else:
            h = rms_matmul(xb, norm_mix[i], w_in_odd, j, False).reshape(bp, tp, 2 * d)
            zeros = jnp.zeros((bp, SUBLANES, d), F32)
            hy, xtail_b, htail_b = rglru(h, rg_conv_w[j], rg_conv_b[j], rg_wa[j], rg_ba[j], rg_wi[j], rg_bi[j],
                                         rg_lambda[j], zeros, zeros, pos0_b, False)
            xb = matmul_residual([hy.reshape(n_b, d)], w_out_odd, j, xb, False)
            h = rms_matmul(xq, norm_mix[i], w_in_odd, j, True).reshape(bq, r8, 2 * d)
            prevx = jnp.concatenate([_pad_rows(state_rg_conv[j], 1, SUBLANES, at_end=False),
                                     xtail_b[:, :SUBLANES]])
            prevh = jnp.concatenate([_pad_rows(state_rg_h[j][:, None, :], 1, SUBLANES, at_end=False),
                                     htail_b[:, :SUBLANES]])
            hy, xtail_q, htail_q = rglru(h, rg_conv_w[j], rg_conv_b[j], rg_wa[j], rg_ba[j], rg_wi[j], rg_bi[j],
                                         rg_lambda[j], prevx, prevh, pos0_q, True)
            xq = matmul_residual([hy.reshape(bq * r8, d)], w_out_odd, j, xq, True)
            new_rc.append(seq_rows(xtail_q, -3, 0))
            new_rh.append(seq_rows(htail_q, -1, 0).reshape(bq, d))

        wcat = jnp.zeros((d, LANES), F32).at[:, :N_GROUPS].set(moe_wg[i]).at[:, N_GROUPS:N_GROUPS + N_EXPERTS].set(moe_we[i])
        bcat = jnp.zeros((1, LANES), F32).at[0, :N_GROUPS].set(moe_bg[i]).at[0, N_GROUPS:N_GROUPS + N_EXPERTS].set(moe_be[i])
        routed = [router(x, norm_ffn[i], wcat, bcat) for x in (xb, xq)]
        xn = jnp.concatenate([r[0] for r in routed], axis=0)
        ids = jnp.concatenate([r[1][:, :2] for r in routed], axis=0)
        wts = jnp.concatenate([r[2][:, :2] for r in routed], axis=0)
        slot, src_tok, gate, tile_expert, n_active = _dispatch_plan(ids, wts, MOE_TM, nt)
        ys = moe_grouped(jnp.take(xn, src_tok, axis=0), gate, tile_expert, n_active,
                         moe_w1, moe_w3, moe_w2, i, MOE_TM)
        y = jnp.take(ys, slot[:, 0], axis=0) + jnp.take(ys, slot[:, 1], axis=0)
        xb = xb + y[:n_b]
        xq = xq + y[n_b:]

        xb = ple(xb, norm_ple[i], ple_gate, pb, ple_proj, i, False)
        xq = ple(xq, norm_ple[i], ple_gate, pq, ple_proj, i, True)

    def heads(a, dv):
        return a.reshape(a.shape[0], a.shape[1], N_HEADS, dv)

    y_prompt = xb.reshape(bp, tp, d)
    y_sample = xq.reshape(bq, r8, d)[:bs, :ts]
    outs = [y_prompt, y_sample]
    for gi, nb in ((0, bp), (1, bs)):
        sl = slice(bs, bs + bp) if gi == 0 else slice(0, bs)
        outs += [jnp.stack([heads(kv[gi], 2 * HEAD_DK) for kv in new_k]),
                 jnp.stack([heads(kv[gi], HEAD_DV) for kv in new_v]),
                 jnp.stack([a[sl] for a in new_sc]),
                 jnp.stack([a[sl] for a in new_rc]),
                 jnp.stack([a[sl] for a in new_rh])]
    return tuple(outs)
```
